```python
import jax, jax.numpy as jnp
from jax import lax
import numpy as np

D_MODEL = 1024
BATCH = 4
SEQ = 4096
DEPTH = 1
DEC_BATCH = 128
DEC_SEQ = 1
PAST_LEN = 8192
PAGE_SIZE = 128

HEAD_DIM = 64
H_FOX = 8
KV_FOX = 4
H_MOBA = 8
KV_MOBA = 4
GROUP = H_FOX // KV_FOX
H_ALL = H_FOX + H_MOBA
KV_ALL = KV_FOX + KV_MOBA
W_FOX = H_FOX * HEAD_DIM
W_MOBA = H_MOBA * HEAD_DIM
ROPE_DIM = HEAD_DIM // 4
ROPE_THETA = 500000.0
MOBA_BLOCK = 256
MOBA_TOPK = 3
FOX_Q_BLOCK = 128
MOBA_Q_BLOCK = 32
NORM_EPS = 1e-6
ATTN_SCALE = HEAD_DIM ** -0.5

OFF_K = H_ALL * HEAD_DIM
OFF_V = OFF_K + KV_ALL * HEAD_DIM
OFF_F = OFF_V + KV_ALL * HEAD_DIM
OFF_ZF = OFF_F + H_FOX
OFF_ZM = OFF_ZF + W_FOX
OFF_G = OFF_ZM + W_MOBA
D_IN = OFF_G + 2 * D_MODEL

kernel_name = 'fox_moba_gated_hybrid_step'

F32 = jnp.float32


def _rms(x, g):
    xf = x.astype(F32)
    y = xf * lax.rsqrt(jnp.mean(xf * xf, axis=-1, keepdims=True) + NORM_EPS)
    return (y * g.astype(F32)).astype(x.dtype)


def _rope(x, pos):
    half = ROPE_DIM // 2
    inv = ROPE_THETA ** (-jnp.arange(half, dtype=F32) / half)
    ang = jnp.asarray(pos).astype(F32)[:, None] * inv[None, :]
    cos = jnp.cos(ang)[:, None, :]
    sin = jnp.sin(ang)[:, None, :]
    xf = x.astype(F32)
    x1 = xf[..., :half]
    x2 = xf[..., half:ROPE_DIM]
    return jnp.concatenate([x1 * cos - x2 * sin, x2 * cos + x1 * sin, xf[..., ROPE_DIM:]], axis=-1).astype(x.dtype)


def _mix_inputs(x, c, pos, g_pre, w_ada, b_ada, w_in, b_f):
    B, S, _ = x.shape
    mod = (c @ w_ada + b_ada)[:, None, :]
    shift, scale, res_gate = jnp.split(mod, 3, axis=-1)
    h = _rms(x, g_pre) * (1 + scale) + shift
    u = h @ w_in
    q = u[..., :OFF_K].reshape(B, S, H_ALL, HEAD_DIM)
    k = u[..., OFF_K:OFF_V].reshape(B, S, KV_ALL, HEAD_DIM)
    v = u[..., OFF_V:OFF_F].reshape(B, S, KV_ALL, HEAD_DIM)
    logf = jax.nn.log_sigmoid(u[..., OFF_F:OFF_ZF].astype(F32) + b_f.astype(F32))
    q_fox = q[:, :, :H_FOX]
    q_moba = _rope(q[:, :, H_FOX:], pos)
    k = jnp.concatenate([k[:, :, :KV_FOX], _rope(k[:, :, KV_FOX:], pos)], axis=2)
    return (q_fox, q_moba, k, v, logf, u[..., OFF_ZF:OFF_ZM], u[..., OFF_ZM:OFF_G], u[..., OFF_G:], res_gate)


def _merge_out(x, o_fox, o_moba, z_fox, z_moba, gate_logits, res_gate, w_up_fox, w_up_moba, w_o, g_post):
    y_fox = (o_fox * jax.nn.silu(z_fox)) @ w_up_fox
    y_moba = (o_moba * jax.nn.silu(z_moba)) @ w_up_moba
    g_fox, g_moba = jnp.split(jax.nn.sigmoid(gate_logits), 2, axis=-1)
    m = (g_fox * y_fox + g_moba * y_moba) @ w_o
    return x + res_gate * _rms(m, g_post)


def _fox_prompt(q, k, v, logf):
    B, S, _, _ = q.shape
    q = q.reshape(B, S, KV_FOX, GROUP, HEAD_DIM)
    cum = jnp.cumsum(logf, axis=1).reshape(B, S, KV_FOX, GROUP).transpose(0, 2, 3, 1)
    key_pos = jnp.arange(S)

    def block(i):
        s0 = i * FOX_Q_BLOCK
        qb = lax.dynamic_slice_in_dim(q, s0, FOX_Q_BLOCK, axis=1)
        cq = lax.dynamic_slice_in_dim(cum, s0, FOX_Q_BLOCK, axis=3)
        s = (jnp.einsum('bqkgd,bskd->bkgqs', qb, k).astype(F32) * ATTN_SCALE
             + cq[..., :, None] - cum[..., None, :])
        causal = (s0 + jnp.arange(FOX_Q_BLOCK))[:, None] >= key_pos[None, :]
        p = jax.nn.softmax(jnp.where(causal, s, -jnp.inf), axis=-1).astype(v.dtype)
        return jnp.einsum('bkgqs,bskd->bqkgd', p, v)

    o = lax.map(block, jnp.arange(S // FOX_Q_BLOCK))
    return jnp.moveaxis(o, 0, 1).reshape(B, S, W_FOX)


def _fox_sample(q, k_new, v_new, logf_new, k_past, v_past, logf_past):
    DB, Q, _, _ = q.shape
    P = k_past.shape[1]
    q = q.reshape(DB, Q, KV_FOX, GROUP, HEAD_DIM)
    cum_past = jnp.cumsum(logf_past.astype(F32), axis=1)
    cum_new = cum_past[:, -1:] + jnp.cumsum(logf_new.astype(F32), axis=1)
    cp = cum_past.reshape(DB, P, KV_FOX, GROUP).transpose(0, 2, 3, 1)
    cn = cum_new.reshape(DB, Q, KV_FOX, GROUP).transpose(0, 2, 3, 1)
    s_past = (jnp.einsum('bqkgd,bskd->bkgqs', q, k_past).astype(F32) * ATTN_SCALE
              + cn[..., :, None] - cp[..., None, :])
    s_new = (jnp.einsum('bqkgd,bskd->bkgqs', q, k_new).astype(F32) * ATTN_SCALE
             + cn[..., :, None] - cn[..., None, :])
    causal = np.arange(Q)[:, None] >= np.arange(Q)[None, :]
    s_new = jnp.where(causal, s_new, -jnp.inf)
    p = jax.nn.softmax(jnp.concatenate([s_past, s_new], axis=-1), axis=-1).astype(v_new.dtype)
    o = (jnp.einsum('bkgqs,bskd->bqkgd', p[..., :P], v_past)
         + jnp.einsum('bkgqs,bskd->bqkgd', p[..., P:], v_new))
    return o.reshape(DB, Q, W_FOX)


def _moba_select(q, means, cur):
    nb = means.shape[1]
    s = jnp.einsum('bqkgd,bnkd->bqkgn', q, means).astype(F32)
    fully_past = jnp.arange(nb)[None, :] < cur[:, None]
    s = jnp.where(fully_past[None, :, None, None, :], s, -jnp.inf)
    _, idx = lax.top_k(s, min(MOBA_TOPK, nb))
    ok = idx < cur[None, :, None, None, None]
    return idx, ok


def _moba_core(q, k_sel, v_sel, ok, k_own, v_own, own_mask, own):
    B, Q, KV, G, N, L, D = k_sel.shape
    s_sel = jnp.einsum('bqkgd,bqkgnld->bqkgnl', q, k_sel).astype(F32) * ATTN_SCALE
    s_sel = jnp.where(ok[..., None], s_sel, -jnp.inf).reshape(B, Q, KV, G, N * L)
    s_own = jnp.einsum('bqkgd,' + own + '->bqkgl', q, k_own).astype(F32) * ATTN_SCALE
    s_own = jnp.where(own_mask[None, :, None, None, :], s_own, -jnp.inf)
    p = jax.nn.softmax(jnp.concatenate([s_sel, s_own], axis=-1), axis=-1).astype(v_own.dtype)
    p_sel = p[..., :N * L].reshape(B, Q, KV, G, N, L)
    p_own = p[..., N * L:]
    return (jnp.einsum('bqkgnl,bqkgnld->bqkgd', p_sel, v_sel)
            + jnp.einsum('bqkgl,' + own + '->bqkgd', p_own, v_own))


def _moba_prompt(q, k, v):
    B, S, _, _ = q.shape
    L = MOBA_BLOCK
    q = q.reshape(B, S, KV_MOBA, GROUP, HEAD_DIM)
    nb = -(-S // L)
    padw = ((0, 0), (0, nb * L - S), (0, 0), (0, 0))
    kp = jnp.pad(k, padw)
    vp = jnp.pad(v, padw)
    kb = kp.reshape(B, nb, L, KV_MOBA, HEAD_DIM)
    means = jnp.mean(kb.astype(F32), axis=2).astype(k.dtype)
    kb_t = jnp.moveaxis(kb, 3, 1)
    vb_t = jnp.moveaxis(vp.reshape(B, nb, L, KV_MOBA, HEAD_DIM), 3, 1)
    b_i = jnp.arange(B)[:, None, None, None, None]
    kv_i = jnp.arange(KV_MOBA)[None, None, :, None, None]

    def block(i):
        s0 = i * MOBA_Q_BLOCK
        pos_q = s0 + jnp.arange(MOBA_Q_BLOCK)
        qb = lax.dynamic_slice_in_dim(q, s0, MOBA_Q_BLOCK, axis=1)
        idx, ok = _moba_select(qb, means, pos_q // L)
        k_sel = kb_t[b_i, kv_i, idx]
        v_sel = vb_t[b_i, kv_i, idx]
        o0 = (s0 // L) * L
        k_own = lax.dynamic_slice_in_dim(kp, o0, L, axis=1)
        v_own = lax.dynamic_slice_in_dim(vp, o0, L, axis=1)
        own_mask = (o0 + jnp.arange(L))[None, :] <= pos_q[:, None]
        return _moba_core(qb, k_sel, v_sel, ok, k_own, v_own, own_mask, 'blkd')

    o = lax.map(block, jnp.arange(S // MOBA_Q_BLOCK))
    return jnp.moveaxis(o, 0, 1).reshape(B, S, W_MOBA)


def _paged_rows(p, b_i, kv_i, cache, layer, new, page_table, past):
    in_past = p < past
    pp = jnp.minimum(p, past - 1)
    phys = page_table[b_i, pp // PAGE_SIZE]
    old = cache[layer, phys, pp % PAGE_SIZE, KV_FOX + kv_i]
    pn = jnp.clip(p - past, 0, new.shape[1] - 1)
    fresh = new[b_i, pn, kv_i]
    return jnp.where(in_past[..., None], old, fresh)


def _moba_sample(q, k_new, v_new, cache_k, cache_v, layer, page_table):
    DB, Q, _, _ = q.shape
    L = MOBA_BLOCK
    n_pages = page_table.shape[1]
    past = n_pages * PAGE_SIZE
    q = q.reshape(DB, Q, KV_MOBA, GROUP, HEAD_DIM)
    k_rows = cache_k[layer, page_table, :, KV_FOX:]
    page_sum = jnp.sum(k_rows.astype(F32), axis=2)
    nb = -(-(past + Q) // L)
    page_blk = jnp.asarray((np.arange(n_pages) * PAGE_SIZE) // L)
    new_blk = jnp.asarray((past + np.arange(Q)) // L)
    sums = (jax.ops.segment_sum(jnp.moveaxis(page_sum, 1, 0), page_blk, num_segments=nb)
            + jax.ops.segment_sum(jnp.moveaxis(k_new.astype(F32), 1, 0), new_blk, num_segments=nb))
    means = (jnp.moveaxis(sums, 0, 1) / L).astype(q.dtype)
    pos_q = past + np.arange(Q)
    idx, ok = _moba_select(q, means, pos_q // L)
    r = jnp.arange(L)
    p_sel = idx[..., None] * L + r
    b6 = jnp.arange(DB)[:, None, None, None, None, None]
    kv6 = jnp.arange(KV_MOBA)[None, None, :, None, None, None]
    k_sel = _paged_rows(p_sel, b6, kv6, cache_k, layer, k_new, page_table, past)
    v_sel = _paged_rows(p_sel, b6, kv6, cache_v, layer, v_new, page_table, past)
    p_own = (pos_q // L * L)[:, None] + np.arange(L)[None, :]
    b4 = jnp.arange(DB)[:, None, None, None]
    kv4 = jnp.arange(KV_MOBA)[None, None, None, :]
    p_own4 = jnp.asarray(p_own)[None, :, :, None]
    k_own = _paged_rows(p_own4, b4, kv4, cache_k, layer, k_new, page_table, past)
    v_own = _paged_rows(p_own4, b4, kv4, cache_v, layer, v_new, page_table, past)
    own_mask = p_own <= pos_q[:, None]
    o = _moba_core(q, k_sel, v_sel, ok, k_own, v_own, own_mask, 'bqlkd')
    return o.reshape(DB, Q, W_MOBA)


def _layer_prompt(x, c, g_pre, w_ada, b_ada, w_in, b_f, w_up_fox, w_up_moba, w_o, g_post):
    S = x.shape[1]
    pos = jnp.arange(S, dtype=F32)
    q_f, q_m, k, v, logf, z_f, z_m, gates, res_gate = _mix_inputs(x, c, pos, g_pre, w_ada, b_ada, w_in, b_f)
    o_f = _fox_prompt(q_f, k[:, :, :KV_FOX], v[:, :, :KV_FOX], logf)
    o_m = _moba_prompt(q_m, k[:, :, KV_FOX:], v[:, :, KV_FOX:])
    y = _merge_out(x, o_f, o_m, z_f, z_m, gates, res_gate, w_up_fox, w_up_moba, w_o, g_post)
    return y, k, v, logf


def _layer_sample(x, c, cache_k, cache_v, cache_logf, layer, page_table,
                  g_pre, w_ada, b_ada, w_in, b_f, w_up_fox, w_up_moba, w_o, g_post):
    DB, Q, _ = x.shape
    past = page_table.shape[1] * PAGE_SIZE
    pos = jnp.asarray(past + np.arange(Q)).astype(F32)
    q_f, q_m, k, v, logf, z_f, z_m, gates, res_gate = _mix_inputs(x, c, pos, g_pre, w_ada, b_ada, w_in, b_f)
    k_past = cache_k[layer, page_table, :, :KV_FOX].reshape(DB, past, KV_FOX, HEAD_DIM)
    v_past = cache_v[layer, page_table, :, :KV_FOX].reshape(DB, past, KV_FOX, HEAD_DIM)
    logf_past = cache_logf[layer, page_table].reshape(DB, past, H_FOX)
    o_f = _fox_sample(q_f, k[:, :, :KV_FOX], v[:, :, :KV_FOX], logf, k_past, v_past, logf_past)
    o_m = _moba_sample(q_m, k[:, :, KV_FOX:], v[:, :, KV_FOX:], cache_k, cache_v, layer, page_table)
    y = _merge_out(x, o_f, o_m, z_f, z_m, gates, res_gate, w_up_fox, w_up_moba, w_o, g_post)
    return y, k, v, logf


def setup_inputs(seed: int = 0) -> dict:
    key = jax.random.key(seed)
    ks = jax.random.split(key, 17)
    n_pages = PAST_LEN // PAGE_SIZE
    n_used = DEC_BATCH * n_pages
    n_pool = n_used + n_used // 4
    nrm = jax.random.normal
    x_prompt = nrm(ks[0], (BATCH, SEQ, D_MODEL), F32)
    x_sample = nrm(ks[1], (DEC_BATCH, DEC_SEQ, D_MODEL), F32)
    cache_k = nrm(ks[2], (DEPTH, n_pool, PAGE_SIZE, KV_ALL, HEAD_DIM), F32)
    cache_v = nrm(ks[3], (DEPTH, n_pool, PAGE_SIZE, KV_ALL, HEAD_DIM), F32)
    cache_logf = jax.nn.log_sigmoid(3.0 + nrm(ks[4], (DEPTH, n_pool, PAGE_SIZE, H_FOX), F32))
    page_table = jax.random.permutation(ks[5], n_pool)[:n_used].reshape(DEC_BATCH, n_pages).astype(jnp.int32)
    c_prompt = nrm(ks[6], (BATCH, D_MODEL), F32)
    c_sample = nrm(ks[7], (DEC_BATCH, D_MODEL), F32)
    w_ada = nrm(ks[8], (DEPTH, D_MODEL, 3 * D_MODEL), F32) * (0.5 * D_MODEL ** -0.5)
    b_ada = nrm(ks[9], (DEPTH, 3 * D_MODEL), F32) * 0.02
    g_pre = 1.0 + 0.02 * nrm(ks[10], (DEPTH, D_MODEL), F32)
    w_in = nrm(ks[11], (DEPTH, D_MODEL, D_IN), F32) * D_MODEL ** -0.5
    b_f = jax.random.uniform(ks[12], (DEPTH, H_FOX), F32, minval=1.0, maxval=5.0)
    w_up_fox = nrm(ks[13], (DEPTH, W_FOX, D_MODEL), F32) * W_FOX ** -0.5
    w_up_moba = nrm(ks[14], (DEPTH, W_MOBA, D_MODEL), F32) * W_MOBA ** -0.5
    w_o = nrm(ks[15], (DEPTH, D_MODEL, D_MODEL), F32) * D_MODEL ** -0.5
    g_post = 1.0 + 0.02 * nrm(ks[16], (DEPTH, D_MODEL), F32)
    return {'x_prompt': x_prompt, 'x_sample': x_sample, 'cache_k': cache_k, 'cache_v': cache_v,
            'cache_logf': cache_logf, 'page_table': page_table, 'c_prompt': c_prompt, 'c_sample': c_sample,
            'w_ada': w_ada, 'b_ada': b_ada, 'g_pre': g_pre, 'w_in': w_in, 'b_f': b_f,
            'w_up_fox': w_up_fox, 'w_up_moba': w_up_moba, 'w_o': w_o, 'g_post': g_post}


def reference(x_prompt, x_sample, cache_k, cache_v, cache_logf, page_table, c_prompt, c_sample,
              w_ada, b_ada, g_pre, w_in, b_f, w_up_fox, w_up_moba, w_o, g_post):
    y_prompt, y_sample = x_prompt, x_sample
    kp_l, vp_l, fp_l, ks_l, vs_l, fs_l = [], [], [], [], [], []
    for layer in range(DEPTH):
        w = (g_pre[layer], w_ada[layer], b_ada[layer], w_in[layer], b_f[layer],
             w_up_fox[layer], w_up_moba[layer], w_o[layer], g_post[layer])
        y_prompt, kp, vp, fp = _layer_prompt(y_prompt, c_prompt, *w)
        y_sample, ksm, vsm, fsm = _layer_sample(y_sample, c_sample, cache_k, cache_v, cache_logf,
                                                layer, page_table, *w)
        kp_l.append(kp)
        vp_l.append(vp)
        fp_l.append(fp)
        ks_l.append(ksm)
        vs_l.append(vsm)
        fs_l.append(fsm)
    k_prompt = jnp.stack(kp_l)
    v_prompt = jnp.stack(vp_l)
    logf_prompt = jnp.stack(fp_l)
    k_sample = jnp.stack(ks_l)
    v_sample = jnp.stack(vs_l)
    logf_sample = jnp.stack(fs_l)
    return (y_prompt, y_sample, k_prompt, v_prompt, logf_prompt, k_sample, v_sample, logf_sample)
```

```python
import functools

import numpy as np
import jax
import jax.numpy as jnp
from jax import lax
from jax.experimental import pallas as pl
from jax.experimental.pallas import tpu as pltpu

F32 = jnp.float32
BF16 = jnp.bfloat16

D_MODEL = 1024
HEAD_DIM = 64
H_FOX = 8
H_ALL = 16
KV_ALL = 8
W_Q = H_ALL * HEAD_DIM
W_KV = KV_ALL * HEAD_DIM
ROPE_HALF = 8
ROPE_THETA = 500000.0
MOBA_BLOCK = 256
MOBA_TOPK = 3
PAGE_SIZE = 128
NORM_EPS = 1e-6
ATTN_SCALE = HEAD_DIM ** -0.5
NEG_BIG = -1e30

LANES = 128
VMEM_LIMIT = 56 * 1024 * 1024

OFF_K = W_Q
OFF_V = OFF_K + W_KV
OFF_F = OFF_V + W_KV
OFF_ZF = OFF_F + H_FOX
OFF_G = OFF_ZF + W_Q

_HEAD_ORDER = np.array([4 * (t // 4) + (0, 2, 1, 3)[t % 4] for t in range(8)])
_PERM = np.array([512 * (c // 512) + 64 * _HEAD_ORDER[(c % 512) // 64] + c % 64 for c in range(W_Q)])
_INV_PERM = np.argsort(_PERM)

C_Q, C_K, C_V, C_Z, C_G, C_F = 0, 1024, 1536, 2048, 3072, 5120
W_COLS = C_F + LANES

TQ = 256
TM_PROMPT = 512
SAMPLE_PAGES_PER_STEP = 8


def _dot(a, b):
    return jnp.dot(a, b, preferred_element_type=F32)


def _dot_nt(a, b):
    return lax.dot_general(a, b, (((1,), (1,)), ((), ())), preferred_element_type=F32)


def _split3(x):
    hi = x.astype(BF16)
    r1 = x - hi.astype(F32)
    mid = r1.astype(BF16)
    lo = (r1 - mid.astype(F32)).astype(BF16)
    return hi, mid, lo


def _ada_kernel(c_ref, w_ref, b_ref, o_ref):
    o_ref[...] = _dot(c_ref[...].astype(BF16), w_ref[...]) + b_ref[...]


def _ada(c_all, w_ada_bf, b_ada):
    rows = c_all.shape[0]
    return pl.pallas_call(
        _ada_kernel,
        out_shape=jax.ShapeDtypeStruct((rows, 3 * D_MODEL), F32),
        compiler_params=pltpu.CompilerParams(vmem_limit_bytes=VMEM_LIMIT),
        name="ada",
    )(c_all, w_ada_bf, b_ada)


def _rope(x, tab):
    c = tab[:, 0:LANES]
    sa = tab[:, LANES:2 * LANES]
    sb = tab[:, 2 * LANES:3 * LANES]
    return (x * c + pltpu.roll(x, LANES - ROPE_HALF, 1) * sa + pltpu.roll(x, ROPE_HALF, 1) * sb)


def _proj_kernel(x_ref, mod_ref, gpre_ref, w_ref, bf_ref, rope_ref,
                 q_ref, kf_ref, vf_ref, kb_ref, vb_ref, lf8_ref, lf_ref, sz_ref, sg_ref, *rest,
                 n_blocks):
    x = x_ref[...]
    shift = mod_ref[0, :, 0:D_MODEL]
    scale = mod_ref[0, :, D_MODEL:2 * D_MODEL]
    y = x * lax.rsqrt(jnp.mean(x * x, axis=-1, keepdims=True) + NORM_EPS) * gpre_ref[...]
    h = (y * (1.0 + scale) + shift).astype(BF16)
    tab = rope_ref[...]

    uq = _dot(h, w_ref[:, C_Q:C_Q + W_Q])
    q_ref[:, 0:512] = uq[:, 0:512].astype(BF16)
    for j in range(4, 8):
        q_ref[:, LANES * j:LANES * (j + 1)] = _rope(uq[:, LANES * j:LANES * (j + 1)], tab).astype(BF16)

    uk = _dot(h, w_ref[:, C_K:C_K + W_KV])
    kf_ref[:, 0:256] = uk[:, 0:256]
    kb_ref[:, 0:256] = uk[:, 0:256].astype(BF16)
    kr = [_rope(uk[:, LANES * j:LANES * (j + 1)], tab) for j in (2, 3)]
    for j in (2, 3):
        kf_ref[:, LANES * j:LANES * (j + 1)] = kr[j - 2]
        kb_ref[:, LANES * j:LANES * (j + 1)] = kr[j - 2].astype(BF16)
    if n_blocks:
        mean_ref = rest[0]
        for blk in range(n_blocks):
            rows = slice(MOBA_BLOCK * blk, MOBA_BLOCK * (blk + 1))
            mean_ref[0, blk:blk + 1, :] = jnp.concatenate(
                [jnp.sum(kr[0][rows], axis=0, keepdims=True),
                 jnp.sum(kr[1][rows], axis=0, keepdims=True)], axis=1) * (1.0 / MOBA_BLOCK)

    uv = _dot(h, w_ref[:, C_V:C_V + W_KV])
    vf_ref[...] = uv
    vb_ref[...] = uv.astype(BF16)

    uz = _dot(h, w_ref[:, C_Z:C_Z + W_Q])
    sz_ref[...] = (uz / (1.0 + jnp.exp(-uz))).astype(BF16)

    ug = _dot(h, w_ref[:, C_G:C_G + 2 * D_MODEL])
    sg_ref[...] = (1.0 / (1.0 + jnp.exp(-ug))).astype(BF16)

    uf = _dot(h, w_ref[:, C_F:C_F + LANES]) + bf_ref[...]
    lf = -(jnp.maximum(-uf, 0.0) + jnp.log1p(jnp.exp(-jnp.abs(uf))))
    lane = lax.broadcasted_iota(jnp.int32, lf.shape, 1)
    lf = jnp.where(lane < H_FOX, lf, 0.0)
    lf_ref[...] = lf
    lf8_ref[...] = lf[:, 0:H_FOX]


def _proj(x2d, mod3, g_pre, w_all, bf_pad, rope_tab, *, tm, tiles_per_group, rope_tiles, with_means):
    n = x2d.shape[0]
    grid = (n // tm,)
    r_mod = mod3.shape[1]
    n_blocks = tm // MOBA_BLOCK if with_means else 0
    row = lambda i: (i, 0)
    const = lambda i: (0, 0)
    in_specs = [
        pl.BlockSpec((tm, D_MODEL), row),
        pl.BlockSpec((1, r_mod, 3 * D_MODEL), lambda i: (i // tiles_per_group, 0, 0)),
        pl.BlockSpec((1, D_MODEL), const),
        pl.BlockSpec((D_MODEL, W_COLS), const, pipeline_mode=pl.Buffered(1)),
        pl.BlockSpec((1, LANES), const),
        pl.BlockSpec((tm, 3 * LANES), lambda i: (i % rope_tiles, 0)),
    ]
    out_shape = [
        jax.ShapeDtypeStruct((n, W_Q), BF16),
        jax.ShapeDtypeStruct((n, W_KV), F32),
        jax.ShapeDtypeStruct((n, W_KV), F32),
        jax.ShapeDtypeStruct((n, W_KV), BF16),
        jax.ShapeDtypeStruct((n, W_KV), BF16),
        jax.ShapeDtypeStruct((n, H_FOX), F32),
        jax.ShapeDtypeStruct((n, LANES), F32),
        jax.ShapeDtypeStruct((n, W_Q), BF16),
        jax.ShapeDtypeStruct((n, 2 * D_MODEL), BF16),
    ]
    out_specs = [
        pl.BlockSpec((tm, W_Q), row), pl.BlockSpec((tm, W_KV), row), pl.BlockSpec((tm, W_KV), row),
        pl.BlockSpec((tm, W_KV), row), pl.BlockSpec((tm, W_KV), row), pl.BlockSpec((tm, H_FOX), row),
        pl.BlockSpec((tm, LANES), row), pl.BlockSpec((tm, W_Q), row), pl.BlockSpec((tm, 2 * D_MODEL), row),
    ]
    if with_means:
        out_shape.append(jax.ShapeDtypeStruct((n // tm, n_blocks, 256), F32))
        out_specs.append(pl.BlockSpec((1, n_blocks, 256), lambda i: (i, 0, 0)))
    return pl.pallas_call(
        functools.partial(_proj_kernel, n_blocks=n_blocks),
        grid=grid, in_specs=in_specs, out_specs=out_specs, out_shape=out_shape,
        compiler_params=pltpu.CompilerParams(dimension_semantics=("arbitrary",),
                                             vmem_limit_bytes=VMEM_LIMIT),
        name="proj",
    )(x2d, mod3, g_pre, w_all, bf_pad, rope_tab)


def _group16_reduce(x, lane, op):
    for d in (1, 2, 4, 8):
        up = pltpu.roll(x, d, 1)
        dn = pltpu.roll(x, LANES - d, 1)
        x = op(x, jnp.where((lane & d) != 0, up, dn))
    return x


def _prep_kernel(lf_ref, qm_ref, mbd_ref, tri_ref, pmat_ref, crow_ref, qa_ref, ka_ref, carry_ref):
    i = pl.program_id(1)
    tb = lf_ref.shape[0]

    @pl.when(i == 0)
    def _():
        carry_ref[...] = jnp.zeros_like(carry_ref)

    lh, lm, ll = _split3(lf_ref[...])
    c3 = _dot(tri_ref[...], jnp.concatenate([lh, lm, ll], axis=1))
    cum = c3[:, 0:LANES] + c3[:, LANES:2 * LANES] + c3[:, 2 * LANES:3 * LANES] + carry_ref[...]
    carry_ref[...] = cum[tb - 1:tb, :]
    ch, cm, cl = _split3(cum)
    placed = _dot(jnp.concatenate([ch, cm, cl], axis=1), pmat_ref[...]) + crow_ref[...]
    qa_ref[:, 0:LANES] = placed[:, 0:LANES].astype(BF16)
    ka_ref[:, 0:LANES] = placed[:, LANES:2 * LANES].astype(BF16)

    sc = _dot(qm_ref[...], mbd_ref[0])
    lane = lax.broadcasted_iota(jnp.int32, (tb, LANES), 1)
    row = lax.broadcasted_iota(jnp.int32, (tb, LANES), 0)
    blk_n = lane & 15
    cur = (i * tb + row) >> 8
    valid = blk_n < cur
    n_f = blk_n.astype(F32)
    x = jnp.where(valid, sc, -jnp.inf)
    taken = jnp.zeros((tb, LANES), F32)
    for _ in range(MOBA_TOPK):
        m = _group16_reduce(x, lane, jnp.maximum)
        first = _group16_reduce(jnp.where(x == m, n_f, 99.0), lane, jnp.minimum)
        pick = n_f == first
        taken = jnp.where(pick, 1.0, taken)
        x = jnp.where(pick, -jnp.inf, x)
    own = blk_n == cur
    bias = jnp.where(own, 0.0, jnp.where(valid, jnp.where(taken > 0.0, 0.0, NEG_BIG), NEG_BIG))
    qa_ref[:, LANES:2 * LANES] = bias.astype(BF16)
    ka_ref[:, LANES:2 * LANES] = jnp.where(own, 1.0, 0.0).astype(BF16)


def _prep(lf128, q_all, mbd, tri, pmat, crow, *, batch, seq, tb):
    n = lf128.shape[0]
    nt = seq // tb
    row = lambda b, i: (b * nt + i, 0)
    const = lambda b, i: (0, 0)
    return pl.pallas_call(
        _prep_kernel,
        grid=(batch, nt),
        in_specs=[
            pl.BlockSpec((tb, LANES), row),
            pl.BlockSpec((tb, 512), lambda b, i: (b * nt + i, 1)),
            pl.BlockSpec((1, 512, LANES), lambda b, i: (b, 0, 0)),
            pl.BlockSpec((tb, tb), const),
            pl.BlockSpec((3 * LANES, 2 * LANES), const),
            pl.BlockSpec((1, 2 * LANES), const),
        ],
        out_specs=[pl.BlockSpec((tb, 2 * LANES), row), pl.BlockSpec((tb, 2 * LANES), row)],
        out_shape=[jax.ShapeDtypeStruct((n, 2 * LANES), BF16), jax.ShapeDtypeStruct((n, 2 * LANES), BF16)],
        scratch_shapes=[pltpu.VMEM((1, LANES), F32)],
        compiler_params=pltpu.CompilerParams(dimension_semantics=("arbitrary", "arbitrary"),
                                             vmem_limit_bytes=VMEM_LIMIT),
        name="prep",
    )(lf128, q_all, mbd, tri, pmat, crow)


def _attn_kernel(q_ref, qa_ref, hm_ref, k_ref, v_ref, ka_ref, o_ref, qs_ref, m_ref, l_ref, acc_ref):
    qi = pl.program_id(2)
    q = q_ref[0].astype(F32)
    qa = qa_ref[0].astype(F32)
    hm = hm_ref[0]
    lane = lax.broadcasted_iota(jnp.int32, (TQ, LANES), 1)
    low = lane < HEAD_DIM
    for r in range(2):
        for g in range(2):
            a = q[:, LANES * g:LANES * (g + 1)]
            am = jnp.where(low, a, 0.0) if r == 0 else jnp.where(low, 0.0, a)
            au = qa * hm[2 * g + r:2 * g + r + 1, :]
            qs_ref[r, TQ * g:TQ * (g + 1), :] = jnp.concatenate([am, au], axis=1).astype(BF16)
    m_ref[...] = jnp.full(m_ref.shape, -jnp.inf, F32)
    l_ref[...] = jnp.zeros(l_ref.shape, F32)
    acc_ref[...] = jnp.zeros(acc_ref.shape, F32)

    def tile(kj, diagonal):
        start = pl.multiple_of(kj * TQ, TQ)
        kcat = jnp.concatenate([k_ref[0, pl.ds(start, TQ), :], ka_ref[0, pl.ds(start, TQ), :]], axis=1)
        vv = v_ref[0, pl.ds(start, TQ), :]
        for r in range(2):
            s = _dot_nt(qs_ref[r], kcat)
            if diagonal:
                rq = lax.broadcasted_iota(jnp.int32, s.shape, 0) & (TQ - 1)
                ck = lax.broadcasted_iota(jnp.int32, s.shape, 1)
                s = jnp.where(ck <= rq, s, -jnp.inf)
            m_prev = m_ref[r]
            m_new = jnp.maximum(m_prev, jnp.max(s, axis=1, keepdims=True))
            alpha = jnp.exp(m_prev - m_new)
            p = jnp.exp(s - m_new)
            l_ref[r] = alpha * l_ref[r] + jnp.sum(p, axis=1, keepdims=True)
            acc_ref[r] = alpha * acc_ref[r] + _dot(p.astype(BF16), vv)
            m_ref[r] = m_new

    def body(kj, carry):
        tile(kj, False)
        return carry

    lax.fori_loop(0, qi, body, 0)
    tile(qi, True)

    o0 = acc_ref[0] / l_ref[0]
    o1 = acc_ref[1] / l_ref[1]
    outs = [jnp.where(low, o0[TQ * g:TQ * (g + 1)], o1[TQ * g:TQ * (g + 1)]) for g in range(2)]
    o_ref[0] = jnp.concatenate(outs, axis=1).astype(BF16)


def _attn(q3, qaug3, hm, k3, v3, kaug3):
    batch, seq, _ = q3.shape
    nq = seq // TQ
    return pl.pallas_call(
        _attn_kernel,
        grid=(batch, 4, nq),
        in_specs=[
            pl.BlockSpec((1, TQ, 256), lambda b, mp, qi: (b, qi, mp)),
            pl.BlockSpec((1, TQ, LANES), lambda b, mp, qi: (b, qi, mp // 2)),
            pl.BlockSpec((1, 4, LANES), lambda b, mp, qi: (mp, 0, 0)),
            pl.BlockSpec((1, seq, LANES), lambda b, mp, qi: (b, 0, mp)),
            pl.BlockSpec((1, seq, LANES), lambda b, mp, qi: (b, 0, mp)),
            pl.BlockSpec((1, seq, LANES), lambda b, mp, qi: (b, 0, mp // 2)),
        ],
        out_specs=pl.BlockSpec((1, TQ, 256), lambda b, mp, qi: (b, qi, mp)),
        out_shape=jax.ShapeDtypeStruct((batch, seq, W_Q), BF16),
        scratch_shapes=[
            pltpu.VMEM((2, 2 * TQ, 2 * LANES), BF16),
            pltpu.VMEM((2, 2 * TQ, 1), F32),
            pltpu.VMEM((2, 2 * TQ, 1), F32),
            pltpu.VMEM((2, 2 * TQ, LANES), F32),
        ],
        compiler_params=pltpu.CompilerParams(dimension_semantics=("arbitrary", "arbitrary", "arbitrary"),
                                             vmem_limit_bytes=VMEM_LIMIT),
        name="attn",
    )(q3, qaug3, hm, k3, v3, kaug3)


def _sattn_kernel(pt_ref, wq_ref, knew_ref, vnew_ref, lfn_ref, lmat_ref, *rest):
    npg = SAMPLE_PAGES_PER_STEP
    kp = rest[0:npg]
    vp = rest[npg:2 * npg]
    lp = rest[2 * npg:3 * npg]
    o_ref = rest[3 * npg]
    m_ref, l_ref, sc_ref, tail_ref, osc_ref = rest[3 * npg + 1:]
    j = pl.program_id(1)
    n_steps = pl.num_programs(1)
    blocks_per_step = npg * PAGE_SIZE // MOBA_BLOCK
    chunk = n_steps - 1 - j

    @pl.when(j == 0)
    def _():
        m_ref[...] = jnp.full(m_ref.shape, -jnp.inf, F32)
        l_ref[...] = jnp.zeros(l_ref.shape, F32)
        sc_ref[...] = jnp.zeros(sc_ref.shape, F32)
        tail_ref[...] = jnp.zeros(tail_ref.shape, F32)

    wq = wq_ref[0]
    wq32 = wq.astype(F32)
    lane = lax.broadcasted_iota(jnp.int32, (H_ALL, LANES), 1)
    for n in reversed(range(blocks_per_step)):
        kblk = jnp.concatenate([kp[2 * n][...], kp[2 * n + 1][...]], axis=0)
        vblk = jnp.concatenate([vp[2 * n][...], vp[2 * n + 1][...]], axis=0)
        lf = jnp.concatenate([lp[2 * n][...], lp[2 * n + 1][...]], axis=1)
        lh, lm, ll = _split3(lf)
        x3 = jnp.concatenate([lh.astype(F32), lm.astype(F32), ll.astype(F32), jnp.zeros_like(lf)],
                             axis=0).astype(BF16)
        r3 = _dot(x3, lmat_ref[...])
        rev = r3[0:8] + r3[8:16] + r3[16:24] + tail_ref[...]
        tail_ref[...] = tail_ref[...] + jnp.sum(lf, axis=1, keepdims=True)
        bias = jnp.concatenate([rev, jnp.zeros_like(rev)], axis=0)

        s = _dot_nt(wq, kblk.astype(BF16)) + bias
        mx = jnp.max(s, axis=1, keepdims=True)
        p = jnp.exp(s - mx)
        nb = chunk * blocks_per_step + n
        osc_ref[nb] = _dot(p.astype(BF16), vblk.astype(BF16))
        here = lane == nb
        m_ref[...] = jnp.where(here, mx, m_ref[...])
        l_ref[...] = jnp.where(here, jnp.sum(p, axis=1, keepdims=True), l_ref[...])
        mean = (jnp.sum(kblk[:, 256:512], axis=0, keepdims=True) * (1.0 / MOBA_BLOCK)).astype(BF16)
        scn = jnp.sum(wq32[:, 256:512] * mean.astype(F32), axis=1, keepdims=True)
        sc_ref[...] = jnp.where(here, scn, sc_ref[...])

    @pl.when(j == n_steps - 1)
    def _():
        n_past = n_steps * blocks_per_step
        knew = knew_ref[0]
        vnew = vnew_ref[0]
        row = lax.broadcasted_iota(jnp.int32, (H_ALL, LANES), 0)
        s_own = jnp.sum(wq32 * knew.astype(BF16).astype(F32), axis=1, keepdims=True)
        m_all = m_ref[...] + lfn_ref[0]
        mean_new = (knew[:, 256:512] * (1.0 / MOBA_BLOCK)).astype(BF16)
        sc_new = jnp.sum(wq32[:, 256:512] * mean_new.astype(F32), axis=1, keepdims=True)
        sc = jnp.where(lane == n_past, sc_new, sc_ref[...])
        valid = lane < n_past
        lane_f = lane.astype(F32)
        x = jnp.where(valid, sc, -jnp.inf)
        taken = jnp.zeros((H_ALL, LANES), F32)
        for _ in range(MOBA_TOPK):
            mtop = jnp.max(x, axis=1, keepdims=True)
            first = jnp.min(jnp.where(x == mtop, lane_f, 999.0), axis=1, keepdims=True)
            pick = lane_f == first
            taken = jnp.where(pick, 1.0, taken)
            x = jnp.where(pick, -jnp.inf, x)
        use = jnp.where(valid, jnp.where(row < H_FOX, 1.0, taken), 0.0) > 0.0
        m_tot = jnp.maximum(jnp.max(jnp.where(use, m_all, -jnp.inf), axis=1, keepdims=True), s_own)
        w = jnp.where(use, jnp.exp(m_all - m_tot), 0.0)
        e_own = jnp.exp(s_own - m_tot)
        den = jnp.sum(w * l_ref[...], axis=1, keepdims=True) + e_own
        num = e_own * vnew
        for nb in range(n_past):
            num = num + w[:, nb:nb + 1] * osc_ref[nb]
        out = num / den
        rowkv = lax.broadcasted_iota(jnp.int32, (H_ALL, HEAD_DIM), 0) >> 1
        res = jnp.zeros((H_ALL, HEAD_DIM), F32)
        for kv in range(KV_ALL):
            res = res + jnp.where(rowkv == kv, out[:, HEAD_DIM * kv:HEAD_DIM * (kv + 1)], 0.0)
        o_ref[0] = res


def _sattn(page_table, wq, knew, vnew, lfn_col, lmat, ck, cv, clt):
    db, n_pages = page_table.shape
    npg = SAMPLE_PAGES_PER_STEP
    n_steps = n_pages // npg
    n_blocks = n_pages * PAGE_SIZE // MOBA_BLOCK

    def page_map(i):
        return lambda b, j, pt: (pt[b, (n_steps - 1 - j) * npg + i], 0, 0)

    per_b = lambda b, j, pt: (b, 0, 0)
    in_specs = [
        pl.BlockSpec((1, H_ALL, W_KV), per_b),
        pl.BlockSpec((1, 1, W_KV), per_b),
        pl.BlockSpec((1, 1, W_KV), per_b),
        pl.BlockSpec((1, H_ALL, 1), per_b),
        pl.BlockSpec((MOBA_BLOCK, MOBA_BLOCK), lambda b, j, pt: (0, 0)),
    ]
    in_specs += [pl.BlockSpec((None, PAGE_SIZE, W_KV), page_map(i)) for i in range(npg)]
    in_specs += [pl.BlockSpec((None, PAGE_SIZE, W_KV), page_map(i)) for i in range(npg)]
    in_specs += [pl.BlockSpec((None, H_FOX, PAGE_SIZE), page_map(i)) for i in range(npg)]
    grid_spec = pltpu.PrefetchScalarGridSpec(
        num_scalar_prefetch=1,
        grid=(db, n_steps),
        in_specs=in_specs,
        out_specs=pl.BlockSpec((1, H_ALL, HEAD_DIM), per_b),
        scratch_shapes=[
            pltpu.VMEM((H_ALL, LANES), F32),
            pltpu.VMEM((H_ALL, LANES), F32),
            pltpu.VMEM((H_ALL, LANES), F32),
            pltpu.VMEM((H_FOX, 1), F32),
            pltpu.VMEM((n_blocks, H_ALL, W_KV), F32),
        ],
    )
    return pl.pallas_call(
        _sattn_kernel,
        grid_spec=grid_spec,
        out_shape=jax.ShapeDtypeStruct((db, H_ALL, HEAD_DIM), F32),
        compiler_params=pltpu.CompilerParams(dimension_semantics=("arbitrary", "arbitrary"),
                                             vmem_limit_bytes=VMEM_LIMIT),
        name="sattn",
    )(page_table, wq, knew, vnew, lfn_col, lmat, *([ck] * npg), *([cv] * npg), *([clt] * npg))


def _merge_kernel(o_ref, sz_ref, sg_ref, x_ref, rg_ref, wuf_ref, wum_ref, wo_ref, gpost_ref, y_ref):
    a = (o_ref[...].astype(F32) * sz_ref[...].astype(F32)).astype(BF16)
    yf = _dot(a[:, 0:512], wuf_ref[...])
    ym = _dot(a[:, 512:1024], wum_ref[...])
    g = sg_ref[...].astype(F32)
    mm = (g[:, 0:D_MODEL] * yf + g[:, D_MODEL:2 * D_MODEL] * ym).astype(BF16)
    m = _dot(mm, wo_ref[...])
    r = m * lax.rsqrt(jnp.mean(m * m, axis=-1, keepdims=True) + NORM_EPS) * gpost_ref[...]
    y_ref[...] = x_ref[...] + rg_ref[0] * r


def _merge(o2d, sz, sg, x2d, mod3, wuf, wum, wo, g_post, *, tm, tiles_per_group):
    n = x2d.shape[0]
    r_mod = mod3.shape[1]
    row = lambda i: (i, 0)
    const = lambda i: (0, 0)
    return pl.pallas_call(
        _merge_kernel,
        grid=(n // tm,),
        in_specs=[
            pl.BlockSpec((tm, W_Q), row),
            pl.BlockSpec((tm, W_Q), row),
            pl.BlockSpec((tm, 2 * D_MODEL), row),
            pl.BlockSpec((tm, D_MODEL), row),
            pl.BlockSpec((1, r_mod, D_MODEL), lambda i: (i // tiles_per_group, 0, 2)),
            pl.BlockSpec((512, D_MODEL), const),
            pl.BlockSpec((512, D_MODEL), const),
            pl.BlockSpec((D_MODEL, D_MODEL), const),
            pl.BlockSpec((1, D_MODEL), const),
        ],
        out_specs=pl.BlockSpec((tm, D_MODEL), row),
        out_shape=jax.ShapeDtypeStruct((n, D_MODEL), F32),
        compiler_params=pltpu.CompilerParams(dimension_semantics=("arbitrary",),
                                             vmem_limit_bytes=VMEM_LIMIT),
        name="merge",
    )(o2d, sz, sg, x2d, mod3, wuf, wum, wo, g_post)


def _rope_table(pos):
    inv = ROPE_THETA ** (-jnp.arange(ROPE_HALF, dtype=F32) / ROPE_HALF)
    ang = pos.astype(F32)[:, None] * inv[None, :]
    l64 = np.arange(LANES) % HEAD_DIM
    cos_l = jnp.cos(ang)[:, l64 % ROPE_HALF]
    sin_l = jnp.sin(ang)[:, l64 % ROPE_HALF]
    c = jnp.where(l64 < 2 * ROPE_HALF, cos_l, 1.0)
    sa = jnp.where(l64 < ROPE_HALF, -sin_l, 0.0)
    sb = jnp.where((l64 >= ROPE_HALF) & (l64 < 2 * ROPE_HALF), sin_l, 0.0)
    return jnp.concatenate([c, sa, sb], axis=1)


def _bias_placement():
    pmat = np.zeros((3 * LANES, 2 * LANES), np.float32)
    crow = np.zeros((1, 2 * LANES), np.float32)
    for piece in range(3):
        for i in range(H_FOX):
            pmat[piece * LANES + i, 8 * piece + i] = 1.0
            pmat[piece * LANES + i, LANES + 24 + 8 * piece + i] = -1.0
    crow[0, 24:48] = 1.0
    crow[0, LANES:LANES + 24] = 1.0
    return jnp.asarray(pmat, BF16), jnp.asarray(crow, F32)


def _head_masks():
    hm = np.zeros((4, 4, LANES), np.float32)
    lane = np.arange(LANES)
    for pair in range(2):
        for slot in range(4):
            t = 4 * pair + slot
            hm[pair, slot] = (lane < 48) & (lane % 8 == _HEAD_ORDER[t])
            hm[2 + pair, slot] = lane // 16 == t
    return jnp.asarray(hm, F32)


def kernel(x_prompt, x_sample, cache_k, cache_v, cache_logf, page_table, c_prompt, c_sample,
           w_ada, b_ada, g_pre, w_in, b_f, w_up_fox, w_up_moba, w_o, g_post):
    depth = w_in.shape[0]
    assert depth == 1, "single layer stack"
    batch, seq, _ = x_prompt.shape
    db, dq, _ = x_sample.shape
    assert dq == 1 and seq % TM_PROMPT == 0
    n_pool = cache_k.shape[1]
    n_pages = page_table.shape[1]
    past = n_pages * PAGE_SIZE
    assert past % MOBA_BLOCK == 0 and n_pages % SAMPLE_PAGES_PER_STEP == 0

    w = w_in[0]
    w_all = jnp.concatenate([
        w[:, 0:OFF_K][:, _PERM] * ATTN_SCALE, w[:, OFF_K:OFF_V], w[:, OFF_V:OFF_F],
        w[:, OFF_ZF:OFF_G][:, _PERM], w[:, OFF_G:],
        jnp.pad(w[:, OFF_F:OFF_ZF], ((0, 0), (0, LANES - H_FOX))),
    ], axis=1).astype(BF16)
    bf_pad = jnp.pad(b_f[0], (0, LANES - H_FOX)).reshape(1, LANES)
    wuf = w_up_fox[0][_PERM[:512]].astype(BF16)
    wum = w_up_moba[0][_PERM[512:] - 512].astype(BF16)
    wo = w_o[0].astype(BF16)
    gpre2 = g_pre[0].reshape(1, D_MODEL)
    gpost2 = g_post[0].reshape(1, D_MODEL)

    pad_rows = (-batch) % 8
    c_all = jnp.concatenate([c_prompt, jnp.zeros((pad_rows, D_MODEL), F32), c_sample], axis=0)
    mod = _ada(c_all, w_ada[0].astype(BF16), b_ada[0].reshape(1, 3 * D_MODEL))
    mod_p = mod[:batch].reshape(batch, 1, 3 * D_MODEL)
    mod_s = mod[batch + pad_rows:].reshape(1, db, 3 * D_MODEL)

    n_p = batch * seq
    xp2 = x_prompt.reshape(n_p, D_MODEL)
    tiles = seq // TM_PROMPT
    rope_p = _rope_table(jnp.arange(seq, dtype=F32))
    (q_p, kf_p, vf_p, kb_p, vb_p, lf8_p, lf_p, sz_p, sg_p, means) = _proj(
        xp2, mod_p, gpre2, w_all, bf_pad, rope_p,
        tm=TM_PROMPT, tiles_per_group=tiles, rope_tiles=tiles, with_means=True)

    n_blk = seq // MOBA_BLOCK
    means4 = means.reshape(batch, n_blk, 4, HEAD_DIM).astype(BF16)
    per_slot = means4[:, :, _HEAD_ORDER // 2, :]
    mbd = jnp.einsum("bntd,tu->btdun", per_slot, jnp.eye(8, dtype=BF16))
    mbd = jnp.pad(mbd, ((0, 0),) * 4 + ((0, 16 - n_blk),)).reshape(batch, 512, LANES)
    tri = jnp.asarray(np.tril(np.ones((TM_PROMPT, TM_PROMPT), np.float32)), BF16)
    pmat, crow = _bias_placement()
    qaug, kaug = _prep(lf_p, q_p, mbd, tri, pmat, crow, batch=batch, seq=seq, tb=TM_PROMPT)

    o_p = _attn(q_p.reshape(batch, seq, W_Q), qaug.reshape(batch, seq, 2 * LANES), _head_masks(),
                kb_p.reshape(batch, seq, W_KV), vb_p.reshape(batch, seq, W_KV),
                kaug.reshape(batch, seq, 2 * LANES))
    y_p = _merge(o_p.reshape(n_p, W_Q), sz_p, sg_p, xp2, mod_p, wuf, wum, wo, gpost2,
                 tm=TM_PROMPT, tiles_per_group=tiles)

    xs2 = x_sample.reshape(db, D_MODEL)
    rope_s = _rope_table(jnp.full((db,), past, F32))
    (q_s, kf_s, vf_s, _, _, lf8_s, _, sz_s, sg_s) = _proj(
        xs2, mod_s, gpre2, w_all, bf_pad, rope_s,
        tm=db, tiles_per_group=1, rope_tiles=1, with_means=False)
    q_nat = q_s[:, _INV_PERM].reshape(db, H_ALL, 1, HEAD_DIM)
    kv_onehot = jnp.asarray(np.arange(H_ALL)[:, None] // 2 == np.arange(KV_ALL)[None, :], BF16)
    wq = (q_nat * kv_onehot[None, :, :, None]).reshape(db, H_ALL, W_KV)
    lfn_col = jnp.pad(lf8_s, ((0, 0), (0, H_ALL - H_FOX))).reshape(db, H_ALL, 1)
    lmat = jnp.asarray(np.tril(np.ones((MOBA_BLOCK, MOBA_BLOCK), np.float32), -1), BF16)
    o_s = _sattn(page_table, wq, kf_s.reshape(db, 1, W_KV), vf_s.reshape(db, 1, W_KV), lfn_col, lmat,
                 cache_k[0].reshape(n_pool, PAGE_SIZE, W_KV), cache_v[0].reshape(n_pool, PAGE_SIZE, W_KV),
                 jnp.swapaxes(cache_logf[0], 1, 2))
    o_s2 = o_s.reshape(db, W_Q)[:, _PERM].astype(BF16)
    y_s = _merge(o_s2, sz_s, sg_s, xs2, mod_s, wuf, wum, wo, gpost2, tm=db, tiles_per_group=1)

    return (y_p.reshape(batch, seq, D_MODEL),
            y_s.reshape(db, 1, D_MODEL),
            kf_p.reshape(1, batch, seq, KV_ALL, HEAD_DIM),
            vf_p.reshape(1, batch, seq, KV_ALL, HEAD_DIM),
            lf8_p.reshape(1, batch, seq, H_FOX),
            kf_s.reshape(1, db, 1, KV_ALL, HEAD_DIM),
            vf_s.reshape(1, db, 1, KV_ALL, HEAD_DIM),
            lf8_s.reshape(1, db, 1, H_FOX))
```

```python
import functools

import numpy as np
import jax
import jax.numpy as jnp
from jax import lax
from jax.experimental import pallas as pl
from jax.experimental.pallas import tpu as pltpu

F32 = jnp.float32
BF16 = jnp.bfloat16

D_MODEL = 1024
HEAD_DIM = 64
H_FOX = 8
H_ALL = 16
KV_ALL = 8
W_Q = H_ALL * HEAD_DIM
W_KV = KV_ALL * HEAD_DIM
ROPE_HALF = 8
ROPE_THETA = 500000.0
MOBA_BLOCK = 256
MOBA_TOPK = 3
PAGE_SIZE = 128
NORM_EPS = 1e-6
ATTN_SCALE = HEAD_DIM ** -0.5
NEG_BIG = -1e30

LANES = 128
VMEM_LIMIT = 56 * 1024 * 1024

OFF_K = W_Q
OFF_V = OFF_K + W_KV
OFF_F = OFF_V + W_KV
OFF_ZF = OFF_F + H_FOX
OFF_G = OFF_ZF + W_Q

_HEAD_ORDER = np.array([4 * (t // 4) + (0, 2, 1, 3)[t % 4] for t in range(8)])
_PERM = np.array([512 * (c // 512) + 64 * _HEAD_ORDER[(c % 512) // 64] + c % 64 for c in range(W_Q)])
_INV_PERM = np.argsort(_PERM)

C_Q, C_K, C_V, C_Z, C_G, C_F = 0, 1024, 1536, 2048, 3072, 5120
W_COLS = C_F + LANES

TQ = 512
TK = 512
TM_PROMPT = 512
SAMPLE_PAGES_PER_STEP = 16


def _dot(a, b):
    return jnp.dot(a, b, preferred_element_type=F32)


def _dot_nt(a, b):
    return lax.dot_general(a, b, (((1,), (1,)), ((), ())), preferred_element_type=F32)


def _split3(x):
    hi = x.astype(BF16)
    r1 = x - hi.astype(F32)
    mid = r1.astype(BF16)
    lo = (r1 - mid.astype(F32)).astype(BF16)
    return hi, mid, lo


def _ada_kernel(c_ref, w_ref, b_ref, o_ref):
    o_ref[...] = _dot(c_ref[...].astype(BF16), w_ref[...]) + b_ref[...]


def _ada(c_all, w_ada_bf, b_ada):
    rows = c_all.shape[0]
    return pl.pallas_call(
        _ada_kernel,
        out_shape=jax.ShapeDtypeStruct((rows, 3 * D_MODEL), F32),
        compiler_params=pltpu.CompilerParams(vmem_limit_bytes=VMEM_LIMIT),
        name="ada",
    )(c_all, w_ada_bf, b_ada)


def _rope(x, tab):
    c = tab[:, 0:LANES]
    sa = tab[:, LANES:2 * LANES]
    sb = tab[:, 2 * LANES:3 * LANES]
    return (x * c + pltpu.roll(x, LANES - ROPE_HALF, 1) * sa + pltpu.roll(x, ROPE_HALF, 1) * sb)


def _proj_kernel(x_ref, mod_ref, gpre_ref, w_ref, bf_ref, rope_ref,
                 q_ref, kf_ref, vf_ref, kb_ref, vb_ref, lf8_ref, lf_ref, sz_ref, sg_ref, *rest,
                 n_blocks):
    x = x_ref[...]
    shift = mod_ref[0, :, 0:D_MODEL]
    scale = mod_ref[0, :, D_MODEL:2 * D_MODEL]
    y = x * lax.rsqrt(jnp.mean(x * x, axis=-1, keepdims=True) + NORM_EPS) * gpre_ref[...]
    h = (y * (1.0 + scale) + shift).astype(BF16)
    tab = rope_ref[...]

    uq = _dot(h, w_ref[:, C_Q:C_Q + W_Q])
    q_ref[:, 0:512] = uq[:, 0:512].astype(BF16)
    for j in range(4, 8):
        q_ref[:, LANES * j:LANES * (j + 1)] = _rope(uq[:, LANES * j:LANES * (j + 1)], tab).astype(BF16)

    uk = _dot(h, w_ref[:, C_K:C_K + W_KV])
    kf_ref[:, 0:256] = uk[:, 0:256]
    kb_ref[:, 0:256] = uk[:, 0:256].astype(BF16)
    kr = [_rope(uk[:, LANES * j:LANES * (j + 1)], tab) for j in (2, 3)]
    for j in (2, 3):
        kf_ref[:, LANES * j:LANES * (j + 1)] = kr[j - 2]
        kb_ref[:, LANES * j:LANES * (j + 1)] = kr[j - 2].astype(BF16)
    if n_blocks:
        mean_ref = rest[0]
        for blk in range(n_blocks):
            rows = slice(MOBA_BLOCK * blk, MOBA_BLOCK * (blk + 1))
            mean_ref[0, blk:blk + 1, :] = jnp.concatenate(
                [jnp.sum(kr[0][rows], axis=0, keepdims=True),
                 jnp.sum(kr[1][rows], axis=0, keepdims=True)], axis=1) * (1.0 / MOBA_BLOCK)

    uv = _dot(h, w_ref[:, C_V:C_V + W_KV])
    vf_ref[...] = uv
    vb_ref[...] = uv.astype(BF16)

    uz = _dot(h, w_ref[:, C_Z:C_Z + W_Q])
    sz_ref[...] = (uz / (1.0 + jnp.exp(-uz))).astype(BF16)

    ug = _dot(h, w_ref[:, C_G:C_G + 2 * D_MODEL])
    sg_ref[...] = (1.0 / (1.0 + jnp.exp(-ug))).astype(BF16)

    uf = _dot(h, w_ref[:, C_F:C_F + LANES]) + bf_ref[...]
    lf = -(jnp.maximum(-uf, 0.0) + jnp.log1p(jnp.exp(-jnp.abs(uf))))
    lane = lax.broadcasted_iota(jnp.int32, lf.shape, 1)
    lf = jnp.where(lane < H_FOX, lf, 0.0)
    lf_ref[...] = lf
    lf8_ref[...] = lf[:, 0:H_FOX]


def _proj(x2d, mod3, g_pre, w_all, bf_pad, rope_tab, *, tm, tiles_per_group, rope_tiles, with_means):
    n = x2d.shape[0]
    grid = (n // tm,)
    r_mod = mod3.shape[1]
    n_blocks = tm // MOBA_BLOCK if with_means else 0
    row = lambda i: (i, 0)
    const = lambda i: (0, 0)
    in_specs = [
        pl.BlockSpec((tm, D_MODEL), row),
        pl.BlockSpec((1, r_mod, 3 * D_MODEL), lambda i: (i // tiles_per_group, 0, 0)),
        pl.BlockSpec((1, D_MODEL), const),
        pl.BlockSpec((D_MODEL, W_COLS), const, pipeline_mode=pl.Buffered(1)),
        pl.BlockSpec((1, LANES), const),
        pl.BlockSpec((tm, 3 * LANES), lambda i: (i % rope_tiles, 0)),
    ]
    out_shape = [
        jax.ShapeDtypeStruct((n, W_Q), BF16),
        jax.ShapeDtypeStruct((n, W_KV), F32),
        jax.ShapeDtypeStruct((n, W_KV), F32),
        jax.ShapeDtypeStruct((n, W_KV), BF16),
        jax.ShapeDtypeStruct((n, W_KV), BF16),
        jax.ShapeDtypeStruct((n, H_FOX), F32),
        jax.ShapeDtypeStruct((n, LANES), F32),
        jax.ShapeDtypeStruct((n, W_Q), BF16),
        jax.ShapeDtypeStruct((n, 2 * D_MODEL), BF16),
    ]
    out_specs = [
        pl.BlockSpec((tm, W_Q), row), pl.BlockSpec((tm, W_KV), row), pl.BlockSpec((tm, W_KV), row),
        pl.BlockSpec((tm, W_KV), row), pl.BlockSpec((tm, W_KV), row), pl.BlockSpec((tm, H_FOX), row),
        pl.BlockSpec((tm, LANES), row), pl.BlockSpec((tm, W_Q), row), pl.BlockSpec((tm, 2 * D_MODEL), row),
    ]
    if with_means:
        out_shape.append(jax.ShapeDtypeStruct((n // tm, n_blocks, 256), F32))
        out_specs.append(pl.BlockSpec((1, n_blocks, 256), lambda i: (i, 0, 0)))
    return pl.pallas_call(
        functools.partial(_proj_kernel, n_blocks=n_blocks),
        grid=grid, in_specs=in_specs, out_specs=out_specs, out_shape=out_shape,
        compiler_params=pltpu.CompilerParams(dimension_semantics=("arbitrary",),
                                             vmem_limit_bytes=VMEM_LIMIT),
        name="proj",
    )(x2d, mod3, g_pre, w_all, bf_pad, rope_tab)


def _group16_reduce(x, lane, op):
    for d in (1, 2, 4, 8):
        up = pltpu.roll(x, d, 1)
        dn = pltpu.roll(x, LANES - d, 1)
        x = op(x, jnp.where((lane & d) != 0, up, dn))
    return x


def _prep_kernel(lf_ref, qm_ref, mbd_ref, tri_ref, pmat_ref, crow_ref, qa_ref, ka_ref, carry_ref):
    i = pl.program_id(1)
    tb = lf_ref.shape[0]

    @pl.when(i == 0)
    def _():
        carry_ref[...] = jnp.zeros_like(carry_ref)

    lh, lm, ll = _split3(lf_ref[...])
    c3 = _dot(tri_ref[...], jnp.concatenate([lh, lm, ll], axis=1))
    cum = c3[:, 0:LANES] + c3[:, LANES:2 * LANES] + c3[:, 2 * LANES:3 * LANES] + carry_ref[...]
    carry_ref[...] = cum[tb - 1:tb, :]
    ch, cm, cl = _split3(cum)
    placed = _dot(jnp.concatenate([ch, cm, cl], axis=1), pmat_ref[...]) + crow_ref[...]
    qa_ref[:, 0:LANES] = placed[:, 0:LANES].astype(BF16)
    ka_ref[:, 0:LANES] = placed[:, LANES:2 * LANES].astype(BF16)

    sc = _dot(qm_ref[...], mbd_ref[0])
    lane = lax.broadcasted_iota(jnp.int32, (tb, LANES), 1)
    row = lax.broadcasted_iota(jnp.int32, (tb, LANES), 0)
    blk_n = lane & 15
    cur = (i * tb + row) >> 8
    valid = blk_n < cur
    n_f = blk_n.astype(F32)
    x = jnp.where(valid, sc, -jnp.inf)
    taken = jnp.zeros((tb, LANES), F32)
    for _ in range(MOBA_TOPK):
        m = _group16_reduce(x, lane, jnp.maximum)
        first = _group16_reduce(jnp.where(x == m, n_f, 99.0), lane, jnp.minimum)
        pick = n_f == first
        taken = jnp.where(pick, 1.0, taken)
        x = jnp.where(pick, -jnp.inf, x)
    own = blk_n == cur
    bias = jnp.where(own, 0.0, jnp.where(valid, jnp.where(taken > 0.0, 0.0, NEG_BIG), NEG_BIG))
    qa_ref[:, LANES:2 * LANES] = bias.astype(BF16)
    ka_ref[:, LANES:2 * LANES] = jnp.where(own, 1.0, 0.0).astype(BF16)


def _prep(lf128, q_all, mbd, tri, pmat, crow, *, batch, seq, tb):
    n = lf128.shape[0]
    nt = seq // tb
    row = lambda b, i: (b * nt + i, 0)
    const = lambda b, i: (0, 0)
    return pl.pallas_call(
        _prep_kernel,
        grid=(batch, nt),
        in_specs=[
            pl.BlockSpec((tb, LANES), row),
            pl.BlockSpec((tb, 512), lambda b, i: (b * nt + i, 1)),
            pl.BlockSpec((1, 512, LANES), lambda b, i: (b, 0, 0)),
            pl.BlockSpec((tb, tb), const),
            pl.BlockSpec((3 * LANES, 2 * LANES), const),
            pl.BlockSpec((1, 2 * LANES), const),
        ],
        out_specs=[pl.BlockSpec((tb, 2 * LANES), row), pl.BlockSpec((tb, 2 * LANES), row)],
        out_shape=[jax.ShapeDtypeStruct((n, 2 * LANES), BF16), jax.ShapeDtypeStruct((n, 2 * LANES), BF16)],
        scratch_shapes=[pltpu.VMEM((1, LANES), F32)],
        compiler_params=pltpu.CompilerParams(dimension_semantics=("arbitrary", "arbitrary"),
                                             vmem_limit_bytes=VMEM_LIMIT),
        name="prep",
    )(lf128, q_all, mbd, tri, pmat, crow)


def _attn_kernel(q_ref, qa_ref, hm_ref, k_ref, v_ref, ka_ref, o_ref,
                 qst_ref, vt_ref, m_ref, l_ref, acc_ref, st_ref):
    qi = pl.program_id(2)
    n_kt = v_ref.shape[1] // TK

    @pl.when(qi == 0)
    def _():
        for j in range(n_kt):
            vt_ref[j] = v_ref[0, TK * j:TK * (j + 1), :].astype(F32).T.astype(BF16)

    q = q_ref[0].astype(F32)
    qa = qa_ref[0].astype(F32)
    hm = hm_ref[0]
    lane = lax.broadcasted_iota(jnp.int32, (TQ, LANES), 1)
    low = lane < HEAD_DIM
    for r in range(2):
        parts = []
        for g in range(2):
            a = q[:, LANES * g:LANES * (g + 1)]
            am = jnp.where(low, a, 0.0) if r == 0 else jnp.where(low, 0.0, a)
            au = qa * hm[2 * g + r:2 * g + r + 1, :]
            parts.append(jnp.concatenate([am, au], axis=1))
        qst_ref[r] = jnp.concatenate(parts, axis=0).T.astype(BF16)
    m_ref[...] = jnp.full(m_ref.shape, -jnp.inf, F32)
    l_ref[...] = jnp.zeros(l_ref.shape, F32)
    acc_ref[...] = jnp.zeros(acc_ref.shape, F32)

    last = lax.div(qi * TQ, TK)

    def scores(kj):
        start = pl.multiple_of(kj * TK, TK)
        kcat = jnp.concatenate([k_ref[0, pl.ds(start, TK), :], ka_ref[0, pl.ds(start, TK), :]], axis=1)
        return [_dot(kcat, qst_ref[r]) for r in range(2)]

    def softmax_pv(kj, sts, diagonal):
        vt = vt_ref[kj]
        for r in range(2):
            st = sts[r]
            if diagonal:
                key = lax.broadcasted_iota(jnp.int32, st.shape, 0) + (kj * TK - qi * TQ)
                qry = lax.broadcasted_iota(jnp.int32, st.shape, 1) & (TQ - 1)
                st = jnp.where(key <= qry, st, -jnp.inf)
            m_prev = m_ref[r]
            m_new = jnp.maximum(m_prev, jnp.max(st, axis=0, keepdims=True))
            alpha = jnp.exp(m_prev - m_new)
            p = jnp.exp(st - m_new)
            l_ref[r] = alpha * l_ref[r] + jnp.sum(p, axis=0, keepdims=True)
            acc_ref[r] = alpha * acc_ref[r] + _dot(vt, p.astype(BF16))
            m_ref[r] = m_new

    def stash(sts):
        for r in range(2):
            st_ref[r] = sts[r]

    def stashed():
        return [st_ref[r] for r in range(2)]

    stash(scores(0))

    def body(i, carry):
        cur = stashed()
        nxt = scores(2 * i + 1)
        softmax_pv(2 * i, cur, False)
        stash(scores(2 * i + 2))
        softmax_pv(2 * i + 1, nxt, False)
        return carry

    lax.fori_loop(0, lax.div(last, 2), body, 0)

    @pl.when(lax.rem(last, 2) == 0)
    def _():
        softmax_pv(last, stashed(), True)

    @pl.when(lax.rem(last, 2) == 1)
    def _():
        cur = stashed()
        nxt = scores(last)
        softmax_pv(last - 1, cur, False)
        softmax_pv(last, nxt, True)

    o0 = acc_ref[0] / l_ref[0]
    o1 = acc_ref[1] / l_ref[1]
    o = jnp.concatenate([o0[0:HEAD_DIM], o1[HEAD_DIM:LANES]], axis=0).T
    o_ref[0] = jnp.concatenate([o[0:TQ], o[TQ:2 * TQ]], axis=1).astype(BF16)


def _attn(q3, qaug3, hm, k3, v3, kaug3):
    batch, seq, _ = q3.shape
    nq = seq // TQ
    return pl.pallas_call(
        _attn_kernel,
        grid=(batch, 4, nq),
        in_specs=[
            pl.BlockSpec((1, TQ, 256), lambda b, mp, qi: (b, qi, mp)),
            pl.BlockSpec((1, TQ, LANES), lambda b, mp, qi: (b, qi, mp // 2)),
            pl.BlockSpec((1, 4, LANES), lambda b, mp, qi: (mp, 0, 0)),
            pl.BlockSpec((1, seq, LANES), lambda b, mp, qi: (b, 0, mp)),
            pl.BlockSpec((1, seq, LANES), lambda b, mp, qi: (b, 0, mp)),
            pl.BlockSpec((1, seq, LANES), lambda b, mp, qi: (b, 0, mp // 2)),
        ],
        out_specs=pl.BlockSpec((1, TQ, 256), lambda b, mp, qi: (b, qi, mp)),
        out_shape=jax.ShapeDtypeStruct((batch, seq, W_Q), BF16),
        scratch_shapes=[
            pltpu.VMEM((2, 2 * LANES, 2 * TQ), BF16),
            pltpu.VMEM((seq // TK, LANES, TK), BF16),
            pltpu.VMEM((2, 1, 2 * TQ), F32),
            pltpu.VMEM((2, 1, 2 * TQ), F32),
            pltpu.VMEM((2, LANES, 2 * TQ), F32),
            pltpu.VMEM((2, TK, 2 * TQ), F32),
        ],
        compiler_params=pltpu.CompilerParams(dimension_semantics=("arbitrary", "arbitrary", "arbitrary"),
                                             vmem_limit_bytes=VMEM_LIMIT),
        name="attn",
    )(q3, qaug3, hm, k3, v3, kaug3)


def _sattn_kernel(pt_ref, wq_ref, knew_ref, vnew_ref, lfn_ref, lmat_ref, *rest):
    npg = SAMPLE_PAGES_PER_STEP
    kp = rest[0:npg]
    vp = rest[npg:2 * npg]
    lp = rest[2 * npg:3 * npg]
    o_ref = rest[3 * npg]
    m_ref, l_ref, sc_ref, tail_ref, osc_ref = rest[3 * npg + 1:]
    j = pl.program_id(1)
    n_steps = pl.num_programs(1)
    blocks_per_step = npg * PAGE_SIZE // MOBA_BLOCK
    chunk = n_steps - 1 - j

    @pl.when(j == 0)
    def _():
        m_ref[...] = jnp.full(m_ref.shape, -jnp.inf, F32)
        l_ref[...] = jnp.zeros(l_ref.shape, F32)
        sc_ref[...] = jnp.zeros(sc_ref.shape, F32)
        tail_ref[...] = jnp.zeros(tail_ref.shape, F32)

    wq = wq_ref[0]
    wq32 = wq.astype(F32)
    lane = lax.broadcasted_iota(jnp.int32, (H_ALL, LANES), 1)
    kt = jnp.concatenate([kp[i][...] for i in range(npg)], axis=1).astype(BF16)
    sqk = _dot(wq, kt)

    lfs = [jnp.concatenate([lp[2 * n][...], lp[2 * n + 1][...]], axis=1) for n in range(blocks_per_step)]
    pieces = []
    for lf in lfs:
        lh, lm, ll = _split3(lf)
        pieces += [lh.astype(F32), lm.astype(F32), ll.astype(F32), jnp.zeros_like(lf)]
    r3 = _dot(jnp.concatenate(pieces, axis=0).astype(BF16), lmat_ref[...])
    tail = tail_ref[...]
    rev = [None] * blocks_per_step
    for n in reversed(range(blocks_per_step)):
        rev[n] = r3[32 * n:32 * n + 8] + r3[32 * n + 8:32 * n + 16] + r3[32 * n + 16:32 * n + 24] + tail
        tail = tail + jnp.sum(lfs[n], axis=1, keepdims=True)
    tail_ref[...] = tail

    for n in range(blocks_per_step):
        sqk_n = sqk[:, MOBA_BLOCK * n:MOBA_BLOCK * (n + 1)]
        s = sqk_n + jnp.concatenate([rev[n], jnp.zeros_like(rev[n])], axis=0)
        mx = jnp.max(s, axis=1, keepdims=True)
        p = jnp.exp(s - mx)
        vt = jnp.concatenate([vp[2 * n][...], vp[2 * n + 1][...]], axis=1).astype(BF16)
        nb = chunk * blocks_per_step + n
        osc_ref[nb] = _dot_nt(p.astype(BF16), vt)
        here = lane == nb
        m_ref[...] = jnp.where(here, mx, m_ref[...])
        l_ref[...] = jnp.where(here, jnp.sum(p, axis=1, keepdims=True), l_ref[...])
        scn = jnp.sum(sqk_n, axis=1, keepdims=True) * (1.0 / MOBA_BLOCK)
        sc_ref[...] = jnp.where(here, scn, sc_ref[...])

    @pl.when(j == n_steps - 1)
    def _():
        n_past = n_steps * blocks_per_step
        knew = knew_ref[0]
        vnew = vnew_ref[0]
        row = lax.broadcasted_iota(jnp.int32, (H_ALL, LANES), 0)
        s_own = jnp.sum(wq32 * knew.astype(BF16).astype(F32), axis=1, keepdims=True)
        m_all = m_ref[...] + lfn_ref[0]
        mean_new = (knew[:, 256:512] * (1.0 / MOBA_BLOCK)).astype(BF16)
        sc_new = jnp.sum(wq32[:, 256:512] * mean_new.astype(F32), axis=1, keepdims=True)
        sc = jnp.where(lane == n_past, sc_new, sc_ref[...])
        valid = lane < n_past
        lane_f = lane.astype(F32)
        x = jnp.where(valid, sc, -jnp.inf)
        taken = jnp.zeros((H_ALL, LANES), F32)
        for _ in range(MOBA_TOPK):
            mtop = jnp.max(x, axis=1, keepdims=True)
            first = jnp.min(jnp.where(x == mtop, lane_f, 999.0), axis=1, keepdims=True)
            pick = lane_f == first
            taken = jnp.where(pick, 1.0, taken)
            x = jnp.where(pick, -jnp.inf, x)
        use = jnp.where(valid, jnp.where(row < H_FOX, 1.0, taken), 0.0) > 0.0
        m_tot = jnp.maximum(jnp.max(jnp.where(use, m_all, -jnp.inf), axis=1, keepdims=True), s_own)
        w = jnp.where(use, jnp.exp(m_all - m_tot), 0.0)
        e_own = jnp.exp(s_own - m_tot)
        den = jnp.sum(w * l_ref[...], axis=1, keepdims=True) + e_own
        num = e_own * vnew
        for nb in range(n_past):
            num = num + w[:, nb:nb + 1] * osc_ref[nb]
        out = num / den
        rowkv = lax.broadcasted_iota(jnp.int32, (H_ALL, HEAD_DIM), 0) >> 1
        res = jnp.zeros((H_ALL, HEAD_DIM), F32)
        for kv in range(KV_ALL):
            res = res + jnp.where(rowkv == kv, out[:, HEAD_DIM * kv:HEAD_DIM * (kv + 1)], 0.0)
        o_ref[0] = res


def _sattn(page_table, wq, knew, vnew, lfn_col, lmat, ck, cv, clt):
    db, n_pages = page_table.shape
    npg = SAMPLE_PAGES_PER_STEP
    n_steps = n_pages // npg
    n_blocks = n_pages * PAGE_SIZE // MOBA_BLOCK

    def page_map(i):
        return lambda b, j, pt: (pt[b, (n_steps - 1 - j) * npg + i], 0, 0)

    per_b = lambda b, j, pt: (b, 0, 0)
    in_specs = [
        pl.BlockSpec((1, H_ALL, W_KV), per_b),
        pl.BlockSpec((1, 1, W_KV), per_b),
        pl.BlockSpec((1, 1, W_KV), per_b),
        pl.BlockSpec((1, H_ALL, 1), per_b),
        pl.BlockSpec((MOBA_BLOCK, MOBA_BLOCK), lambda b, j, pt: (0, 0)),
    ]
    in_specs += [pl.BlockSpec((None, W_KV, PAGE_SIZE), page_map(i)) for i in range(npg)]
    in_specs += [pl.BlockSpec((None, W_KV, PAGE_SIZE), page_map(i)) for i in range(npg)]
    in_specs += [pl.BlockSpec((None, H_FOX, PAGE_SIZE), page_map(i)) for i in range(npg)]
    grid_spec = pltpu.PrefetchScalarGridSpec(
        num_scalar_prefetch=1,
        grid=(db, n_steps),
        in_specs=in_specs,
        out_specs=pl.BlockSpec((1, H_ALL, HEAD_DIM), per_b),
        scratch_shapes=[
            pltpu.VMEM((H_ALL, LANES), F32),
            pltpu.VMEM((H_ALL, LANES), F32),
            pltpu.VMEM((H_ALL, LANES), F32),
            pltpu.VMEM((H_FOX, 1), F32),
            pltpu.VMEM((n_blocks, H_ALL, W_KV), F32),
        ],
    )
    return pl.pallas_call(
        _sattn_kernel,
        grid_spec=grid_spec,
        out_shape=jax.ShapeDtypeStruct((db, H_ALL, HEAD_DIM), F32),
        compiler_params=pltpu.CompilerParams(dimension_semantics=("arbitrary", "arbitrary"),
                                             vmem_limit_bytes=VMEM_LIMIT),
        name="sattn",
    )(page_table, wq, knew, vnew, lfn_col, lmat, *([ck] * npg), *([cv] * npg), *([clt] * npg))


def _merge_kernel(o_ref, sz_ref, sg_ref, x_ref, rg_ref, wuf_ref, wum_ref, wo_ref, gpost_ref, y_ref):
    a = (o_ref[...].astype(F32) * sz_ref[...].astype(F32)).astype(BF16)
    yf = _dot(a[:, 0:512], wuf_ref[...])
    ym = _dot(a[:, 512:1024], wum_ref[...])
    g = sg_ref[...].astype(F32)
    mm = (g[:, 0:D_MODEL] * yf + g[:, D_MODEL:2 * D_MODEL] * ym).astype(BF16)
    m = _dot(mm, wo_ref[...])
    r = m * lax.rsqrt(jnp.mean(m * m, axis=-1, keepdims=True) + NORM_EPS) * gpost_ref[...]
    y_ref[...] = x_ref[...] + rg_ref[0] * r


def _merge(o2d, sz, sg, x2d, mod3, wuf, wum, wo, g_post, *, tm, tiles_per_group):
    n = x2d.shape[0]
    r_mod = mod3.shape[1]
    row = lambda i: (i, 0)
    const = lambda i: (0, 0)
    return pl.pallas_call(
        _merge_kernel,
        grid=(n // tm,),
        in_specs=[
            pl.BlockSpec((tm, W_Q), row),
            pl.BlockSpec((tm, W_Q), row),
            pl.BlockSpec((tm, 2 * D_MODEL), row),
            pl.BlockSpec((tm, D_MODEL), row),
            pl.BlockSpec((1, r_mod, D_MODEL), lambda i: (i // tiles_per_group, 0, 2)),
            pl.BlockSpec((512, D_MODEL), const),
            pl.BlockSpec((512, D_MODEL), const),
            pl.BlockSpec((D_MODEL, D_MODEL), const),
            pl.BlockSpec((1, D_MODEL), const),
        ],
        out_specs=pl.BlockSpec((tm, D_MODEL), row),
        out_shape=jax.ShapeDtypeStruct((n, D_MODEL), F32),
        compiler_params=pltpu.CompilerParams(dimension_semantics=("arbitrary",),
                                             vmem_limit_bytes=VMEM_LIMIT),
        name="merge",
    )(o2d, sz, sg, x2d, mod3, wuf, wum, wo, g_post)


def _rope_table(pos):
    inv = ROPE_THETA ** (-jnp.arange(ROPE_HALF, dtype=F32) / ROPE_HALF)
    ang = pos.astype(F32)[:, None] * inv[None, :]
    l64 = np.arange(LANES) % HEAD_DIM
    cos_l = jnp.cos(ang)[:, l64 % ROPE_HALF]
    sin_l = jnp.sin(ang)[:, l64 % ROPE_HALF]
    c = jnp.where(l64 < 2 * ROPE_HALF, cos_l, 1.0)
    sa = jnp.where(l64 < ROPE_HALF, -sin_l, 0.0)
    sb = jnp.where((l64 >= ROPE_HALF) & (l64 < 2 * ROPE_HALF), sin_l, 0.0)
    return jnp.concatenate([c, sa, sb], axis=1)


def _bias_placement():
    pmat = np.zeros((3 * LANES, 2 * LANES), np.float32)
    crow = np.zeros((1, 2 * LANES), np.float32)
    for piece in range(3):
        for i in range(H_FOX):
            pmat[piece * LANES + i, 8 * piece + i] = 1.0
            pmat[piece * LANES + i, LANES + 24 + 8 * piece + i] = -1.0
    crow[0, 24:48] = 1.0
    crow[0, LANES:LANES + 24] = 1.0
    return jnp.asarray(pmat, BF16), jnp.asarray(crow, F32)


def _head_masks():
    hm = np.zeros((4, 4, LANES), np.float32)
    lane = np.arange(LANES)
    for pair in range(2):
        for slot in range(4):
            t = 4 * pair + slot
            hm[pair, slot] = (lane < 48) & (lane % 8 == _HEAD_ORDER[t])
            hm[2 + pair, slot] = lane // 16 == t
    return jnp.asarray(hm, F32)


def kernel(x_prompt, x_sample, cache_k, cache_v, cache_logf, page_table, c_prompt, c_sample,
           w_ada, b_ada, g_pre, w_in, b_f, w_up_fox, w_up_moba, w_o, g_post):
    depth = w_in.shape[0]
    assert depth == 1, "single layer stack"
    batch, seq, _ = x_prompt.shape
    db, dq, _ = x_sample.shape
    assert dq == 1 and seq % TM_PROMPT == 0
    n_pool = cache_k.shape[1]
    n_pages = page_table.shape[1]
    past = n_pages * PAGE_SIZE
    assert past % MOBA_BLOCK == 0 and n_pages % SAMPLE_PAGES_PER_STEP == 0

    w = w_in[0]
    w_all = jnp.concatenate([
        w[:, 0:OFF_K][:, _PERM] * ATTN_SCALE, w[:, OFF_K:OFF_V], w[:, OFF_V:OFF_F],
        w[:, OFF_ZF:OFF_G][:, _PERM], w[:, OFF_G:],
        jnp.pad(w[:, OFF_F:OFF_ZF], ((0, 0), (0, LANES - H_FOX))),
    ], axis=1).astype(BF16)
    bf_pad = jnp.pad(b_f[0], (0, LANES - H_FOX)).reshape(1, LANES)
    wuf = w_up_fox[0][_PERM[:512]].astype(BF16)
    wum = w_up_moba[0][_PERM[512:] - 512].astype(BF16)
    wo = w_o[0].astype(BF16)
    gpre2 = g_pre[0].reshape(1, D_MODEL)
    gpost2 = g_post[0].reshape(1, D_MODEL)

    pad_rows = (-batch) % 8
    c_all = jnp.concatenate([c_prompt, jnp.zeros((pad_rows, D_MODEL), F32), c_sample], axis=0)
    mod = _ada(c_all, w_ada[0].astype(BF16), b_ada[0].reshape(1, 3 * D_MODEL))
    mod_p = mod[:batch].reshape(batch, 1, 3 * D_MODEL)
    mod_s = mod[batch + pad_rows:].reshape(1, db, 3 * D_MODEL)

    n_p = batch * seq
    xp2 = x_prompt.reshape(n_p, D_MODEL)
    tiles = seq // TM_PROMPT
    rope_p = _rope_table(jnp.arange(seq, dtype=F32))
    (q_p, kf_p, vf_p, kb_p, vb_p, lf8_p, lf_p, sz_p, sg_p, means) = _proj(
        xp2, mod_p, gpre2, w_all, bf_pad, rope_p,
        tm=TM_PROMPT, tiles_per_group=tiles, rope_tiles=tiles, with_means=True)

    n_blk = seq // MOBA_BLOCK
    means4 = means.reshape(batch, n_blk, 4, HEAD_DIM).astype(BF16)
    per_slot = means4[:, :, _HEAD_ORDER // 2, :]
    mbd = jnp.einsum("bntd,tu->btdun", per_slot, jnp.eye(8, dtype=BF16))
    mbd = jnp.pad(mbd, ((0, 0),) * 4 + ((0, 16 - n_blk),)).reshape(batch, 512, LANES)
    tri = jnp.asarray(np.tril(np.ones((TM_PROMPT, TM_PROMPT), np.float32)), BF16)
    pmat, crow = _bias_placement()
    qaug, kaug = _prep(lf_p, q_p, mbd, tri, pmat, crow, batch=batch, seq=seq, tb=TM_PROMPT)

    o_p = _attn(q_p.reshape(batch, seq, W_Q), qaug.reshape(batch, seq, 2 * LANES), _head_masks(),
                kb_p.reshape(batch, seq, W_KV), vb_p.reshape(batch, seq, W_KV),
                kaug.reshape(batch, seq, 2 * LANES))
    y_p = _merge(o_p.reshape(n_p, W_Q), sz_p, sg_p, xp2, mod_p, wuf, wum, wo, gpost2,
                 tm=TM_PROMPT, tiles_per_group=tiles)

    xs2 = x_sample.reshape(db, D_MODEL)
    rope_s = _rope_table(jnp.full((db,), past, F32))
    (q_s, kf_s, vf_s, _, _, lf8_s, _, sz_s, sg_s) = _proj(
        xs2, mod_s, gpre2, w_all, bf_pad, rope_s,
        tm=db, tiles_per_group=1, rope_tiles=1, with_means=False)
    q_nat = q_s[:, _INV_PERM].reshape(db, H_ALL, 1, HEAD_DIM)
    kv_onehot = jnp.asarray(np.arange(H_ALL)[:, None] // 2 == np.arange(KV_ALL)[None, :], BF16)
    wq = (q_nat * kv_onehot[None, :, :, None]).reshape(db, H_ALL, W_KV)
    lfn_col = jnp.pad(lf8_s, ((0, 0), (0, H_ALL - H_FOX))).reshape(db, H_ALL, 1)
    lmat = jnp.asarray(np.tril(np.ones((MOBA_BLOCK, MOBA_BLOCK), np.float32), -1), BF16)
    ckt = jnp.transpose(cache_k[0], (0, 2, 3, 1)).reshape(n_pool, W_KV, PAGE_SIZE)
    cvt = jnp.transpose(cache_v[0], (0, 2, 3, 1)).reshape(n_pool, W_KV, PAGE_SIZE)
    o_s = _sattn(page_table, wq, kf_s.reshape(db, 1, W_KV), vf_s.reshape(db, 1, W_KV), lfn_col, lmat,
                 ckt, cvt, jnp.swapaxes(cache_logf[0], 1, 2))
    o_s2 = o_s.reshape(db, W_Q)[:, _PERM].astype(BF16)
    y_s = _merge(o_s2, sz_s, sg_s, xs2, mod_s, wuf, wum, wo, gpost2, tm=db, tiles_per_group=1)

    return (y_p.reshape(batch, seq, D_MODEL),
            y_s.reshape(db, 1, D_MODEL),
            kf_p.reshape(1, batch, seq, KV_ALL, HEAD_DIM),
            vf_p.reshape(1, batch, seq, KV_ALL, HEAD_DIM),
            lf8_p.reshape(1, batch, seq, H_FOX),
            kf_s.reshape(1, db, 1, KV_ALL, HEAD_DIM),
            vf_s.reshape(1, db, 1, KV_ALL, HEAD_DIM),
            lf8_s.reshape(1, db, 1, H_FOX))
```
